```python
import math
import jax, jax.numpy as jnp
from jax import lax
import numpy as np

D_MODEL = 1024
BATCH = 8
SEQ = 8192
DEPTH = 2

CHUNK = 64
N_META = 16
Q_BLOCK = 128
N_A_LAYERS = DEPTH // 2
N_B_LAYERS = DEPTH - N_A_LAYERS
D_FF = 2816
FFN_RES = 0.5
RMS_EPS = 1e-6
SSM_EXPAND = 2
D_INNER = SSM_EXPAND * D_MODEL
SSM_HEADDIM = 64
SSM_HEADS = D_INNER // SSM_HEADDIM
SSM_GROUPS = 8
SSM_HPG = SSM_HEADS // SSM_GROUPS
D_STATE = 128
D_CONV = 4
CONV_DIM = D_INNER + 2 * SSM_GROUPS * D_STATE
IN_PROJ_DIM = D_INNER + CONV_DIM + SSM_HEADS
SB_HEADS = 16
SB_HEAD_DIM = D_MODEL // SB_HEADS

kernel_name = "yoco_mamba2_stickbreaking_macaron"


def rms_norm(x, g):
    x32 = x.astype(jnp.float32)
    y = x32 * lax.rsqrt(jnp.mean(x32 * x32, axis=-1, keepdims=True) + RMS_EPS)
    return (y * g.astype(jnp.float32)).astype(x.dtype)


def head_rms(t, g):
    t32 = t.astype(jnp.float32)
    return t32 * lax.rsqrt(jnp.mean(t32 * t32, axis=-1, keepdims=True) + RMS_EPS) * g.astype(jnp.float32)


def swiglu(x, w1, w3, w2):
    return (jax.nn.silu(x @ w1) * (x @ w3)) @ w2


def causal_depthwise_conv(u, w, b):
    y = lax.conv_general_dilated(u, w[:, None, :].astype(u.dtype), window_strides=(1,),
                                 padding=[(w.shape[0] - 1, 0)],
                                 dimension_numbers=('NWC', 'WIO', 'NWC'),
                                 feature_group_count=u.shape[-1])
    return y + b.astype(u.dtype)


def ssd_chunk_scan(xs, dt, A, bm, cm):
    b, L = xs.shape[:2]
    nc = L // CHUNK

    def to_chunks(t):
        return jnp.moveaxis(t.reshape((b, nc, CHUNK) + t.shape[2:]), 1, 0)

    causal = jnp.tril(jnp.ones((CHUNK, CHUNK), dtype=bool))

    def step(state, inp):
        xc, dtc, bc, cc = inp
        cum = jnp.cumsum(dtc * A, axis=1)
        seg = cum[:, :, None] - cum[:, None, :]
        decay = jnp.exp(jnp.where(causal[None, :, :, None, None], seg, -jnp.inf))
        cb = jnp.einsum('btgn,bsgn->btsg', cc, bc)
        m = cb[..., None] * decay * dtc[:, None]
        y_diag = jnp.einsum('btsgh,bsghp->btghp', m, xc)
        y_off = jnp.einsum('btgn,bghpn->btghp', cc, state) * jnp.exp(cum)[..., None]
        w_end = jnp.exp(cum[:, -1:] - cum) * dtc
        new_state = (state * jnp.exp(cum[:, -1])[..., None, None]
                     + jnp.einsum('bsgn,bsgh,bsghp->bghpn', bc, w_end, xc))
        return new_state, y_diag + y_off

    state0 = jnp.zeros((b, SSM_GROUPS, SSM_HPG, SSM_HEADDIM, D_STATE), jnp.float32)
    _, ys = lax.scan(step, state0, (to_chunks(xs), to_chunks(dt), to_chunks(bm), to_chunks(cm)))
    return jnp.moveaxis(ys, 0, 1).reshape(xs.shape)


def mamba2_mixer(u, in_proj, conv_w, conv_b, dt_bias, a_log, d_skip, norm_g, out_proj):
    b, L, _ = u.shape
    f32 = jnp.float32
    zxbcdt = u @ in_proj
    z, xbc, dt_raw = jnp.split(zxbcdt, [D_INNER, D_INNER + CONV_DIM], axis=-1)
    xbc = jax.nn.silu(causal_depthwise_conv(xbc, conv_w, conv_b))
    xs, bm, cm = jnp.split(xbc, [D_INNER, D_INNER + SSM_GROUPS * D_STATE], axis=-1)
    dt = jax.nn.softplus(dt_raw.astype(f32) + dt_bias.astype(f32)).reshape(b, L, SSM_GROUPS, SSM_HPG)
    A = -jnp.exp(a_log.astype(f32)).reshape(SSM_GROUPS, SSM_HPG)
    xs = xs.astype(f32).reshape(b, L, SSM_GROUPS, SSM_HPG, SSM_HEADDIM)
    bm = bm.astype(f32).reshape(b, L, SSM_GROUPS, D_STATE)
    cm = cm.astype(f32).reshape(b, L, SSM_GROUPS, D_STATE)
    front = (-N_META) % CHUNK
    back = (-(L + front)) % CHUNK

    def pad(t):
        return jnp.pad(t, [(0, 0), (front, back)] + [(0, 0)] * (t.ndim - 2))

    y = ssd_chunk_scan(pad(xs), pad(dt), A, pad(bm), pad(cm))[:, front:front + L]
    y = y + d_skip.astype(f32).reshape(SSM_GROUPS, SSM_HPG)[..., None] * xs
    y = y.reshape(b, L, D_INNER) * jax.nn.silu(z.astype(f32))
    yg = y.reshape(b, L, SSM_GROUPS, D_INNER // SSM_GROUPS)
    yg = yg * lax.rsqrt(jnp.mean(yg * yg, axis=-1, keepdims=True) + RMS_EPS)
    y = (yg.reshape(b, L, D_INNER) * norm_g.astype(f32)).astype(u.dtype)
    return y @ out_proj


def stick_breaking_attention(q, k, v):
    L = q.shape[1]
    edges = [0] + list(range(N_META, L, Q_BLOCK)) + [L]
    scale = SB_HEAD_DIM ** -0.5
    outs = []
    for q0, q1 in zip(edges[:-1], edges[1:]):
        z = jnp.einsum('bqhd,bkhd->bhqk', q[:, q0:q1], k[:, :q1]) * scale
        strict = jnp.arange(q1)[None, :] < jnp.arange(q0, q1)[:, None]
        log_keep = jnp.where(strict, jax.nn.log_sigmoid(-z), 0.0)
        log_w = jax.nn.log_sigmoid(z) + lax.cumsum(log_keep, axis=3, reverse=True) - log_keep
        w = jnp.where(strict, jnp.exp(log_w), 0.0)
        outs.append(jnp.einsum('bhqk,bkhd->bqhd', w, v[:, :q1]))
    return jnp.concatenate(outs, axis=1)


def stick_breaking_mixer(u, w_q, q_norm_g, k_shared, v_shared, w_o):
    b, L, _ = u.shape
    q = head_rms((u @ w_q).reshape(b, L, SB_HEADS, SB_HEAD_DIM), q_norm_g)
    o = stick_breaking_attention(q, k_shared, v_shared)
    return o.reshape(b, L, D_MODEL).astype(u.dtype) @ w_o


def setup_inputs(seed: int = 0) -> dict:
    key = jax.random.key(seed)
    ks = jax.random.split(key, 24)
    f32 = jnp.float32

    def nrm(k, shape, fan_in):
        return jax.random.normal(k, shape, f32) * fan_in ** -0.5

    def gain(k, shape):
        return 1.0 + 0.02 * jax.random.normal(k, shape, f32)

    dt0 = jnp.exp(jax.random.uniform(ks[9], (N_A_LAYERS, SSM_HEADS), f32, math.log(1e-3), math.log(1e-1)))
    dt_bias = dt0 + jnp.log(-jnp.expm1(-dt0))
    return {
        "x": jax.random.normal(ks[0], (BATCH, SEQ, D_MODEL), f32),
        "meta_tokens": jax.random.normal(ks[1], (N_META, D_MODEL), f32),
        "norm_g": gain(ks[2], (DEPTH, 3, D_MODEL)),
        "ffn_w1": nrm(ks[3], (DEPTH, 2, D_MODEL, D_FF), D_MODEL),
        "ffn_w3": nrm(ks[4], (DEPTH, 2, D_MODEL, D_FF), D_MODEL),
        "ffn_w2": nrm(ks[5], (DEPTH, 2, D_FF, D_MODEL), D_FF),
        "ssm_in_proj": nrm(ks[6], (N_A_LAYERS, D_MODEL, IN_PROJ_DIM), D_MODEL),
        "ssm_conv_w": nrm(ks[7], (N_A_LAYERS, D_CONV, CONV_DIM), D_CONV),
        "ssm_conv_b": 0.02 * jax.random.normal(ks[8], (N_A_LAYERS, CONV_DIM), f32),
        "ssm_dt_bias": dt_bias,
        "ssm_a_log": jnp.log(jax.random.uniform(ks[10], (N_A_LAYERS, SSM_HEADS), f32, 1.0, 16.0)),
        "ssm_d": 1.0 + 0.1 * jax.random.normal(ks[11], (N_A_LAYERS, SSM_HEADS), f32),
        "ssm_norm_g": gain(ks[12], (N_A_LAYERS, D_INNER)),
        "ssm_out_proj": nrm(ks[13], (N_A_LAYERS, D_INNER, D_MODEL), D_INNER),
        "kv_norm_g": gain(ks[14], (D_MODEL,)),
        "w_k": nrm(ks[15], (D_MODEL, D_MODEL), D_MODEL),
        "k_norm_g": gain(ks[16], (SB_HEAD_DIM,)),
        "w_v": nrm(ks[17], (D_MODEL, D_MODEL), D_MODEL),
        "sb_w_q": nrm(ks[18], (N_B_LAYERS, D_MODEL, D_MODEL), D_MODEL),
        "sb_q_norm_g": gain(ks[19], (N_B_LAYERS, SB_HEAD_DIM)),
        "sb_w_o": nrm(ks[20], (N_B_LAYERS, D_MODEL, D_MODEL), D_MODEL),
    }


def reference(x, meta_tokens, norm_g, ffn_w1, ffn_w3, ffn_w2, ssm_in_proj, ssm_conv_w, ssm_conv_b,
              ssm_dt_bias, ssm_a_log, ssm_d, ssm_norm_g, ssm_out_proj, kv_norm_g, w_k, k_norm_g, w_v,
              sb_w_q, sb_q_norm_g, sb_w_o):
    b = x.shape[0]
    meta = jnp.broadcast_to(meta_tokens[None].astype(x.dtype), (b, N_META, D_MODEL))
    h = jnp.concatenate([meta, x], axis=1)
    L = h.shape[1]
    k_shared = None
    v_shared = None
    for i in range(DEPTH):
        h = h + FFN_RES * swiglu(rms_norm(h, norm_g[i, 0]), ffn_w1[i, 0], ffn_w3[i, 0], ffn_w2[i, 0])
        u = rms_norm(h, norm_g[i, 1])
        if i < N_A_LAYERS:
            h = h + mamba2_mixer(u, ssm_in_proj[i], ssm_conv_w[i], ssm_conv_b[i], ssm_dt_bias[i],
                                 ssm_a_log[i], ssm_d[i], ssm_norm_g[i], ssm_out_proj[i])
        else:
            j = i - N_A_LAYERS
            h = h + stick_breaking_mixer(u, sb_w_q[j], sb_q_norm_g[j], k_shared, v_shared, sb_w_o[j])
        h = h + FFN_RES * swiglu(rms_norm(h, norm_g[i, 2]), ffn_w1[i, 1], ffn_w3[i, 1], ffn_w2[i, 1])
        if i == N_A_LAYERS - 1:
            kv_in = rms_norm(h, kv_norm_g)
            k_shared = head_rms((kv_in @ w_k).reshape(b, L, SB_HEADS, SB_HEAD_DIM), k_norm_g)
            v_shared = (kv_in @ w_v).reshape(b, L, SB_HEADS, SB_HEAD_DIM).astype(jnp.float32)
    return h[:, N_META:]
```

```python
import functools

import jax
import jax.numpy as jnp
from jax import lax
from jax.experimental import pallas as pl
from jax.experimental.pallas import tpu as pltpu

F32 = jnp.float32
BF16 = jnp.bfloat16

RMS_EPS = 1e-6
FFN_RES = 0.5
N_META = 16
D_CONV = 4
SSM_HEADDIM = 64
SSM_GROUPS = 8
D_STATE = 128
SB_HEAD_DIM = 64

LANES = 128
SUBLANES = 8
SSD_CHUNK = 128
ATTN_BLOCK = 256
ROW_TILE = 512
VMEM_LIMIT = 56 * 1024 * 1024

LOG_W_ZERO = -104.0


def _row_tile(rows, target=ROW_TILE):
    t = min(target, rows)
    while rows % t:
        t -= ATTN_BLOCK
    return t


def _params(*sem):
    return pltpu.CompilerParams(dimension_semantics=sem, vmem_limit_bytes=VMEM_LIMIT)


def _resident(shape):
    nd = len(shape)
    return pl.BlockSpec(shape, lambda *_: (0,) * nd, pipeline_mode=pl.Buffered(1))


def _rms_bf16(x, g):
    ms = jnp.mean(x * x, axis=-1, keepdims=True)
    return (x * lax.rsqrt(ms + RMS_EPS) * g).astype(BF16)


def _split_bf16(x):
    hi = x.astype(BF16)
    lo = (x - hi.astype(F32)).astype(BF16)
    return hi, lo


def _dot(a, b):
    return jnp.dot(a, b, preferred_element_type=F32)


def _dot_nt(a, b):
    return lax.dot_general(a, b, (((1,), (1,)), ((), ())), preferred_element_type=F32)


def _dot_tn(a, b):
    return lax.dot_general(a, b, (((0,), (0,)), ((), ())), preferred_element_type=F32)


def _ffn_body(h_ref, g_ref, w1_ref, w3_ref, w2_ref, o_ref):
    x = h_ref[...]
    xn = _rms_bf16(x, g_ref[...])
    a = _dot(xn, w1_ref[...])
    b = _dot(xn, w3_ref[...])
    gate = (a * jax.nn.sigmoid(a) * b).astype(BF16)
    o_ref[...] = x + FFN_RES * _dot(gate, w2_ref[...])


def _ffn(h, g, w1, w3, w2):
    rows, d = h.shape
    tm = _row_tile(rows)
    row_spec = pl.BlockSpec((tm, d), lambda i: (i, 0))
    return pl.pallas_call(
        _ffn_body,
        grid=(rows // tm,),
        in_specs=[row_spec, _resident(g.shape), _resident(w1.shape), _resident(w3.shape),
                  _resident(w2.shape)],
        out_specs=row_spec,
        out_shape=jax.ShapeDtypeStruct(h.shape, F32),
        compiler_params=_params("parallel"),
        name="ffn",
    )(h, g, w1, w3, w2)


def _norm_proj_body(h_ref, g_ref, w_ref, o_ref, xn_ref):
    @pl.when(pl.program_id(1) == 0)
    def _():
        xn_ref[...] = _rms_bf16(h_ref[...], g_ref[...])

    o_ref[...] = _dot(xn_ref[...], w_ref[...])


def _norm_proj(h, g, w, tn):
    rows, d = h.shape
    n = w.shape[1]
    tm = _row_tile(rows, 2 * ROW_TILE)
    return pl.pallas_call(
        _norm_proj_body,
        grid=(rows // tm, n // tn),
        in_specs=[pl.BlockSpec((tm, d), lambda i, j: (i, 0)), _resident(g.shape),
                  pl.BlockSpec((d, tn), lambda i, j: (0, j))],
        out_specs=pl.BlockSpec((tm, tn), lambda i, j: (i, j)),
        out_shape=jax.ShapeDtypeStruct((rows, n), F32),
        scratch_shapes=[pltpu.VMEM((tm, d), BF16)],
        compiler_params=_params("parallel", "arbitrary"),
        name="norm_proj",
    )(h, g, w)


def _head_rms(t, pool, gain):
    parts = []
    for p in range(t.shape[1] // LANES):
        tp = t[:, p * LANES:(p + 1) * LANES]
        hi, lo = _split_bf16(tp * tp)
        ms = _dot(hi, pool) + _dot(lo, pool)
        parts.append(tp * lax.rsqrt(ms + RMS_EPS) * gain)
    return jnp.concatenate(parts, axis=1)


def _q_proj_body(h_ref, g_ref, w_ref, pool_ref, gain_ref, o_ref):
    q = _dot(_rms_bf16(h_ref[...], g_ref[...]), w_ref[...])
    o_ref[...] = _head_rms(q, pool_ref[...], gain_ref[...]).astype(BF16)


def _q_proj(h, g, w, pool, gain):
    rows, d = h.shape
    tm = _row_tile(rows, 2 * ROW_TILE)
    row_spec = pl.BlockSpec((tm, d), lambda i: (i, 0))
    return pl.pallas_call(
        _q_proj_body,
        grid=(rows // tm,),
        in_specs=[row_spec, _resident(g.shape), _resident(w.shape), _resident(pool.shape),
                  _resident(gain.shape)],
        out_specs=row_spec,
        out_shape=jax.ShapeDtypeStruct((rows, w.shape[1]), BF16),
        compiler_params=_params("parallel"),
        name="q_proj",
    )(h, g, w, pool, gain)


def _kv_proj_body(h_ref, g_ref, wk_ref, wv_ref, pool_ref, gain_ref, k_ref, v_ref):
    xn = _rms_bf16(h_ref[...], g_ref[...])
    k_ref[...] = _head_rms(_dot(xn, wk_ref[...]), pool_ref[...], gain_ref[...]).astype(BF16)
    v_ref[...] = _dot(xn, wv_ref[...]).astype(BF16)


def _kv_proj(h, g, wk, wv, pool, gain):
    rows, d = h.shape
    tm = _row_tile(rows, 2 * ROW_TILE)
    row_spec = pl.BlockSpec((tm, d), lambda i: (i, 0))
    out = jax.ShapeDtypeStruct((rows, wk.shape[1]), BF16)
    return pl.pallas_call(
        _kv_proj_body,
        grid=(rows // tm,),
        in_specs=[row_spec, _resident(g.shape), _resident(wk.shape), _resident(wv.shape),
                  _resident(pool.shape), _resident(gain.shape)],
        out_specs=[row_spec, row_spec],
        out_shape=[out, out],
        compiler_params=_params("parallel"),
        name="kv_proj",
    )(h, g, wk, wv, pool, gain)


def _out_proj_body(o_ref, h_ref, w_ref, out_ref):
    out_ref[...] = h_ref[...] + _dot(o_ref[...], w_ref[...])


def _out_proj(o, h, w):
    rows, d = h.shape
    tm = _row_tile(rows, 2 * ROW_TILE)
    return pl.pallas_call(
        _out_proj_body,
        grid=(rows // tm,),
        in_specs=[pl.BlockSpec((tm, o.shape[1]), lambda i: (i, 0)),
                  pl.BlockSpec((tm, d), lambda i: (i, 0)), _resident(w.shape)],
        out_specs=pl.BlockSpec((tm, d), lambda i: (i, 0)),
        out_shape=jax.ShapeDtypeStruct(h.shape, F32),
        compiler_params=_params("parallel"),
        name="out_proj",
    )(o, h, w)


def _ssd_body(x_ref, bc_ref, dt_ref, cw_ref, cb_ref, dtb_ref, alog_ref, dexp_ref, expand_ref,
              y_ref, state_ref, ubuf_ref, xc_ref):
    q = x_ref.shape[1]
    d_inner = x_ref.shape[2]
    gw = d_inner // SSM_GROUPS
    hpg = gw // SSM_HEADDIM

    @pl.when(pl.program_id(1) == 0)
    def _():
        state_ref[...] = jnp.zeros_like(state_ref)
        ubuf_ref[0:SUBLANES, :] = jnp.zeros((SUBLANES, ubuf_ref.shape[1]), F32)

    ubuf_ref[SUBLANES:SUBLANES + q, 0:d_inner] = x_ref[0]
    ubuf_ref[SUBLANES:SUBLANES + q, d_inner:] = bc_ref[0]
    acc = cb_ref[...] + cw_ref[0:1, :] * ubuf_ref[SUBLANES - 3:SUBLANES - 3 + q, :]
    for k in range(1, D_CONV):
        lo = SUBLANES - (D_CONV - 1) + k
        acc = acc + cw_ref[k:k + 1, :] * ubuf_ref[lo:lo + q, :]
    xc_ref[...] = acc * jax.nn.sigmoid(acc)
    ubuf_ref[0:SUBLANES, :] = ubuf_ref[q:q + SUBLANES, :]

    v = dt_ref[0] + dtb_ref[...]
    dt = jnp.maximum(v, 0.0) + jnp.log1p(jnp.exp(-jnp.abs(v)))
    a_neg = -jnp.exp(alog_ref[...])
    row = lax.broadcasted_iota(jnp.int32, (q, LANES), 0)
    cum = dt * a_neg
    s = 1
    while s < q:
        cum = cum + jnp.where(row >= s, pltpu.roll(cum, s, axis=0), 0.0)
        s *= 2
    cum_t = cum.T
    dt_t = dt.T
    cum_end = cum[q - 1:q, :]
    ecum_hi, ecum_lo = _split_bf16(jnp.exp(cum))
    wend_hi, wend_lo = _split_bf16(jnp.exp(cum_end - cum) * dt)
    expand = expand_ref[...]
    ecum_x = _dot(ecum_hi, expand) + _dot(ecum_lo, expand)
    wend_x = _dot(wend_hi, expand) + _dot(wend_lo, expand)

    tri = (lax.broadcasted_iota(jnp.int32, (q, q), 0)
           >= lax.broadcasted_iota(jnp.int32, (q, q), 1))
    lane_g = lax.broadcasted_iota(jnp.int32, (q, gw), 1)

    for g in range(SSM_GROUPS):
        xg = xc_ref[:, g * gw:(g + 1) * gw]
        bg = xc_ref[:, d_inner + g * D_STATE:d_inner + (g + 1) * D_STATE].astype(BF16)
        cg = xc_ref[:, d_inner + (SSM_GROUPS + g) * D_STATE:
                    d_inner + (SSM_GROUPS + g + 1) * D_STATE].astype(BF16)
        cb = _dot_nt(cg, bg)
        xg_b = xg.astype(BF16)
        m_parts, x_parts = [], []
        for hh in range(hpg):
            h = g * hpg + hh
            seg = cum[:, h:h + 1] - cum_t[h:h + 1, :]
            decay = jnp.where(tri, jnp.exp(jnp.minimum(seg, 0.0)), 0.0)
            m_parts.append((cb * decay * dt_t[h:h + 1, :]).astype(BF16))
            in_head = (lane_g >= hh * SSM_HEADDIM) & (lane_g < (hh + 1) * SSM_HEADDIM)
            x_parts.append(jnp.where(in_head, xg_b, jnp.zeros_like(xg_b)))
        y_diag = _dot(jnp.concatenate(m_parts, axis=1), jnp.concatenate(x_parts, axis=0))
        st = state_ref[g]
        ecum_g = ecum_x[:, g * gw:(g + 1) * gw]
        y_off = _dot(cg, st.astype(BF16)) * ecum_g
        xw = (xg * wend_x[:, g * gw:(g + 1) * gw]).astype(BF16)
        state_ref[g] = st * ecum_g[q - 1:q, :] + _dot_tn(bg, xw)
        y_ref[0, :, g * gw:(g + 1) * gw] = y_diag + y_off + dexp_ref[:, g * gw:(g + 1) * gw] * xg


def _ssd(zx, dt_raw, conv_w, conv_b, dt_bias, a_log, d_exp, expand, d_inner):
    b, lp, _ = zx.shape
    conv_dim = conv_w.shape[1]
    q = SSD_CHUNK
    gw = d_inner // SSM_GROUPS
    col = pl.BlockSpec((1, q, d_inner), lambda i, c: (i, c, 1))
    col2 = pl.BlockSpec((1, q, conv_dim - d_inner), lambda i, c: (i, c, 2))
    return pl.pallas_call(
        _ssd_body,
        grid=(b, lp // q),
        in_specs=[col, col2, pl.BlockSpec((1, q, LANES), lambda i, c: (i, c, 0)),
                  _resident(conv_w.shape), _resident(conv_b.shape), _resident(dt_bias.shape),
                  _resident(a_log.shape), _resident(d_exp.shape), _resident(expand.shape)],
        out_specs=pl.BlockSpec((1, q, d_inner), lambda i, c: (i, c, 0)),
        out_shape=jax.ShapeDtypeStruct((b, lp, d_inner), F32),
        scratch_shapes=[pltpu.VMEM((SSM_GROUPS, D_STATE, gw), F32),
                        pltpu.VMEM((q + SUBLANES, conv_dim), F32),
                        pltpu.VMEM((q, conv_dim), F32)],
        compiler_params=_params("parallel", "arbitrary"),
        name="ssd",
    )(zx, zx, dt_raw, conv_w, conv_b, dt_bias, a_log, d_exp, expand)


def _gate_out_body(y_ref, z_ref, h_ref, g_ref, w_ref, o_ref):
    z = z_ref[...]
    y = y_ref[...] * (z * jax.nn.sigmoid(z))
    gw = y.shape[1] // SSM_GROUPS
    parts = []
    for g in range(SSM_GROUPS):
        yg = y[:, g * gw:(g + 1) * gw]
        ms = jnp.mean(yg * yg, axis=-1, keepdims=True)
        parts.append(yg * lax.rsqrt(ms + RMS_EPS))
    yn = (jnp.concatenate(parts, axis=1) * g_ref[...]).astype(BF16)
    o_ref[...] = h_ref[...] + _dot(yn, w_ref[...])


def _gate_out(y, zx, h, norm_g, w):
    rows, d = h.shape
    d_inner = y.shape[1]
    tm = _row_tile(rows)
    return pl.pallas_call(
        _gate_out_body,
        grid=(rows // tm,),
        in_specs=[pl.BlockSpec((tm, d_inner), lambda i: (i, 0)),
                  pl.BlockSpec((tm, d_inner), lambda i: (i, 0)),
                  pl.BlockSpec((tm, d), lambda i: (i, 0)), _resident(norm_g.shape),
                  _resident(w.shape)],
        out_specs=pl.BlockSpec((tm, d), lambda i: (i, 0)),
        out_shape=jax.ShapeDtypeStruct(h.shape, F32),
        compiler_params=_params("parallel"),
        name="gate_out",
    )(y, zx, h, norm_g, w)


def _attn_body(q_ref, k_ref, v_ref, o_ref, r_ref, acc_ref):
    blk = q_ref.shape[1]
    i = pl.program_id(2)
    q = q_ref[0]
    lane = lax.broadcasted_iota(jnp.int32, (blk, LANES), 1)
    q_heads = (jnp.where(lane < SB_HEAD_DIM, q, jnp.zeros_like(q)),
               jnp.where(lane >= SB_HEAD_DIM, q, jnp.zeros_like(q)))
    rr = lax.broadcasted_iota(jnp.int32, (blk, blk), 0)
    cc = lax.broadcasted_iota(jnp.int32, (blk, blk), 1)
    later = (rr > cc).astype(BF16)
    ones = jnp.ones((blk, LANES), BF16)
    r_ref[...] = jnp.zeros_like(r_ref)
    acc_ref[...] = jnp.zeros_like(acc_ref)

    def sweep(j, diagonal):
        start = pl.multiple_of(j * blk, blk)
        kb = k_ref[0, pl.ds(start, blk), :]
        vb = v_ref[0, pl.ds(start, blk), :]
        for hd in range(2):
            z = _dot_nt(q_heads[hd], kb)
            sp = jnp.maximum(z, 0.0) + jnp.log1p(jnp.exp(-jnp.abs(z)))
            log_keep = -sp
            if diagonal:
                log_keep = jnp.where(cc < rr, log_keep, 0.0)
            hi, lo = _split_bf16(log_keep)
            after = _dot(hi, later) + _dot(lo, later)
            total = _dot(hi, ones) + _dot(lo, ones)
            r = r_ref[hd]
            log_w = (z - sp) + after + jnp.concatenate([r] * (blk // LANES), axis=1)
            w = jnp.exp(log_w)
            if diagonal:
                w = jnp.where(cc < rr, w, 0.0)
            acc_ref[hd] += _dot(w.astype(BF16), vb)
            r_ref[hd] = r + total

    sweep(i, True)

    def cond(carry):
        j, go = carry
        return jnp.logical_and(j >= 0, go)

    def body(carry):
        j, _ = carry
        sweep(j, False)
        return j - 1, jnp.max(r_ref[...]) > LOG_W_ZERO

    lax.while_loop(cond, body, (i - 1, True))
    o_ref[0] = jnp.where(lane < SB_HEAD_DIM, acc_ref[0], acc_ref[1]).astype(BF16)


def _attention(q, k, v):
    b, lp, d = q.shape
    blk = ATTN_BLOCK
    blk_spec = pl.BlockSpec((1, blk, LANES), lambda bi, p, i: (bi, i, p))
    seq_spec = pl.BlockSpec((1, lp, LANES), lambda bi, p, i: (bi, 0, p))
    return pl.pallas_call(
        _attn_body,
        grid=(b, d // LANES, lp // blk),
        in_specs=[blk_spec, seq_spec, seq_spec],
        out_specs=blk_spec,
        out_shape=jax.ShapeDtypeStruct(q.shape, BF16),
        scratch_shapes=[pltpu.VMEM((2, blk, LANES), F32), pltpu.VMEM((2, blk, LANES), F32)],
        compiler_params=_params("parallel", "parallel", "arbitrary"),
        name="stick_breaking",
    )(q, k, v)


def kernel(x, meta_tokens, norm_g, ffn_w1, ffn_w3, ffn_w2, ssm_in_proj, ssm_conv_w, ssm_conv_b,
           ssm_dt_bias, ssm_a_log, ssm_d, ssm_norm_g, ssm_out_proj, kv_norm_g, w_k, k_norm_g, w_v,
           sb_w_q, sb_q_norm_g, sb_w_o):
    b, seq, d = x.shape
    depth = norm_g.shape[0]
    n_a = ssm_in_proj.shape[0]
    d_inner = ssm_out_proj.shape[1]
    conv_dim = ssm_conv_w.shape[2]
    n_heads = ssm_dt_bias.shape[1]
    l = N_META + seq
    lp = -(-l // ATTN_BLOCK) * ATTN_BLOCK
    rows = b * lp

    meta = jnp.broadcast_to(meta_tokens[None].astype(x.dtype), (b, N_META, d))
    h = jnp.concatenate([meta, x, jnp.zeros((b, lp - l, d), x.dtype)], axis=1).reshape(rows, d)

    def ffn(h, i, j):
        return _ffn(h, norm_g[i, j * 2][None], ffn_w1[i, j].astype(BF16), ffn_w3[i, j].astype(BF16),
                    ffn_w2[i, j].astype(BF16))

    def pad_lanes(t):
        return jnp.pad(t, [(0, 0), (0, LANES - t.shape[1])])

    head_of_lane = jnp.arange(LANES) // SB_HEAD_DIM
    pool = ((head_of_lane[:, None] == head_of_lane[None, :]).astype(F32) / SB_HEAD_DIM).astype(BF16)
    expand = (jnp.arange(LANES)[:, None] == jnp.arange(d_inner)[None, :] // SSM_HEADDIM).astype(BF16)

    k_shared = v_shared = None
    for i in range(depth):
        h = ffn(h, i, 0)
        g_mix = norm_g[i, 1][None]
        if i < n_a:
            w_in = ssm_in_proj[i]
            zx = _norm_proj(h, g_mix, w_in[:, :d_inner + conv_dim].astype(BF16), d_inner)
            dt_raw = _norm_proj(h, g_mix, pad_lanes(w_in[:, d_inner + conv_dim:]).astype(BF16), LANES)
            y = _ssd(zx.reshape(b, lp, -1), dt_raw.reshape(b, lp, LANES), ssm_conv_w[i],
                     ssm_conv_b[i][None], pad_lanes(ssm_dt_bias[i][None]), pad_lanes(ssm_a_log[i][None]),
                     jnp.repeat(ssm_d[i], SSM_HEADDIM)[None], expand, d_inner)
            h = _gate_out(y.reshape(rows, d_inner), zx, h, ssm_norm_g[i][None],
                          ssm_out_proj[i].astype(BF16))
        else:
            j = i - n_a
            q_gain = jnp.tile(sb_q_norm_g[j], LANES // SB_HEAD_DIM)[None] * SB_HEAD_DIM ** -0.5
            q = _q_proj(h, g_mix, sb_w_q[j].astype(BF16), pool, q_gain)
            o = _attention(q.reshape(b, lp, d), k_shared, v_shared)
            h = _out_proj(o.reshape(rows, d), h, sb_w_o[j].astype(BF16))
        h = ffn(h, i, 1)
        if i == n_a - 1:
            k_gain = jnp.tile(k_norm_g, LANES // SB_HEAD_DIM)[None]
            k2, v2 = _kv_proj(h, kv_norm_g[None], w_k.astype(BF16), w_v.astype(BF16), pool, k_gain)
            k_shared, v_shared = k2.reshape(b, lp, d), v2.reshape(b, lp, d)
    return h.reshape(b, lp, d)[:, N_META:l]
```

```python
import functools

import jax
import jax.numpy as jnp
from jax import lax
from jax.experimental import pallas as pl
from jax.experimental.pallas import tpu as pltpu

F32 = jnp.float32
BF16 = jnp.bfloat16

RMS_EPS = 1e-6
FFN_RES = 0.5
N_META = 16
D_CONV = 4
SSM_HEADDIM = 64
SSM_GROUPS = 8
D_STATE = 128
SB_HEAD_DIM = 64

LANES = 128
SUBLANES = 8
SSD_CHUNK = 128
ATTN_BLOCK = 256
ATTN_LANES = 512
ROW_TILE = 512
VMEM_LIMIT = 56 * 1024 * 1024

ATTN_LAG = 1
LOG_W_ZERO = -104.0
LOG2E = 1.4426950408889634


def _row_tile(rows, target=ROW_TILE):
    t = min(target, rows)
    while rows % t:
        t -= ATTN_BLOCK
    return t


def _params(*sem):
    return pltpu.CompilerParams(dimension_semantics=sem, vmem_limit_bytes=VMEM_LIMIT)


def _resident(shape):
    nd = len(shape)
    return pl.BlockSpec(shape, lambda *_: (0,) * nd, pipeline_mode=pl.Buffered(1))


def _rms_bf16(x, g):
    ms = jnp.mean(x * x, axis=-1, keepdims=True)
    return (x * lax.rsqrt(ms + RMS_EPS) * g).astype(BF16)


def _split_bf16(x):
    hi = x.astype(BF16)
    lo = (x - hi.astype(F32)).astype(BF16)
    return hi, lo


def _dot(a, b):
    return jnp.dot(a, b, preferred_element_type=F32)


def _dot_nt(a, b):
    return lax.dot_general(a, b, (((1,), (1,)), ((), ())), preferred_element_type=F32)


def _dot_tn(a, b):
    return lax.dot_general(a, b, (((0,), (0,)), ((), ())), preferred_element_type=F32)


def _ffn_body(h_ref, g_ref, w1_ref, w3_ref, w2_ref, o_ref):
    x = h_ref[...]
    xn = _rms_bf16(x, g_ref[...])
    a = _dot(xn, w1_ref[...])
    b = _dot(xn, w3_ref[...])
    gate = (a * jax.nn.sigmoid(a) * b).astype(BF16)
    o_ref[...] = x + FFN_RES * _dot(gate, w2_ref[...])


def _ffn(h, g, w1, w3, w2):
    rows, d = h.shape
    tm = _row_tile(rows)
    row_spec = pl.BlockSpec((tm, d), lambda i: (i, 0))
    return pl.pallas_call(
        _ffn_body,
        grid=(rows // tm,),
        in_specs=[row_spec, _resident(g.shape), _resident(w1.shape), _resident(w3.shape),
                  _resident(w2.shape)],
        out_specs=row_spec,
        out_shape=jax.ShapeDtypeStruct(h.shape, F32),
        compiler_params=_params("parallel"),
        name="ffn",
    )(h, g, w1, w3, w2)


def _norm_proj_body(h_ref, g_ref, w_ref, o_ref, xn_ref):
    @pl.when(pl.program_id(1) == 0)
    def _():
        xn_ref[...] = _rms_bf16(h_ref[...], g_ref[...])

    o_ref[...] = _dot(xn_ref[...], w_ref[...])


def _norm_proj(h, g, w, tn):
    rows, d = h.shape
    n = w.shape[1]
    tm = _row_tile(rows, 2 * ROW_TILE)
    return pl.pallas_call(
        _norm_proj_body,
        grid=(rows // tm, n // tn),
        in_specs=[pl.BlockSpec((tm, d), lambda i, j: (i, 0)), _resident(g.shape),
                  pl.BlockSpec((d, tn), lambda i, j: (0, j))],
        out_specs=pl.BlockSpec((tm, tn), lambda i, j: (i, j)),
        out_shape=jax.ShapeDtypeStruct((rows, n), F32),
        scratch_shapes=[pltpu.VMEM((tm, d), BF16)],
        compiler_params=_params("parallel", "arbitrary"),
        name="norm_proj",
    )(h, g, w)


def _head_rms(t, pool, gain):
    parts = []
    for p in range(t.shape[1] // LANES):
        tp = t[:, p * LANES:(p + 1) * LANES]
        hi, lo = _split_bf16(tp * tp)
        ms = _dot(hi, pool) + _dot(lo, pool)
        parts.append(tp * lax.rsqrt(ms + RMS_EPS) * gain)
    return jnp.concatenate(parts, axis=1)


def _q_proj_body(h_ref, g_ref, w_ref, pool_ref, gain_ref, o_ref):
    q = _dot(_rms_bf16(h_ref[...], g_ref[...]), w_ref[...])
    o_ref[...] = _head_rms(q, pool_ref[...], gain_ref[...]).astype(BF16)


def _q_proj(h, g, w, pool, gain):
    rows, d = h.shape
    tm = _row_tile(rows, 2 * ROW_TILE)
    row_spec = pl.BlockSpec((tm, d), lambda i: (i, 0))
    return pl.pallas_call(
        _q_proj_body,
        grid=(rows // tm,),
        in_specs=[row_spec, _resident(g.shape), _resident(w.shape), _resident(pool.shape),
                  _resident(gain.shape)],
        out_specs=row_spec,
        out_shape=jax.ShapeDtypeStruct((rows, w.shape[1]), BF16),
        compiler_params=_params("parallel"),
        name="q_proj",
    )(h, g, w, pool, gain)


def _kv_proj_body(h_ref, g_ref, wk_ref, wv_ref, pool_ref, gain_ref, k_ref, v_ref):
    xn = _rms_bf16(h_ref[...], g_ref[...])
    k_ref[...] = _head_rms(_dot(xn, wk_ref[...]), pool_ref[...], gain_ref[...]).astype(BF16)
    v_ref[...] = _dot(xn, wv_ref[...]).astype(BF16)


def _kv_proj(h, g, wk, wv, pool, gain):
    rows, d = h.shape
    tm = _row_tile(rows, 2 * ROW_TILE)
    row_spec = pl.BlockSpec((tm, d), lambda i: (i, 0))
    out = jax.ShapeDtypeStruct((rows, wk.shape[1]), BF16)
    return pl.pallas_call(
        _kv_proj_body,
        grid=(rows // tm,),
        in_specs=[row_spec, _resident(g.shape), _resident(wk.shape), _resident(wv.shape),
                  _resident(pool.shape), _resident(gain.shape)],
        out_specs=[row_spec, row_spec],
        out_shape=[out, out],
        compiler_params=_params("parallel"),
        name="kv_proj",
    )(h, g, wk, wv, pool, gain)


def _out_proj_body(o_ref, h_ref, w_ref, out_ref):
    out_ref[...] = h_ref[...] + _dot(o_ref[...], w_ref[...])


def _out_proj(o, h, w):
    rows, d = h.shape
    tm = _row_tile(rows, 2 * ROW_TILE)
    return pl.pallas_call(
        _out_proj_body,
        grid=(rows // tm,),
        in_specs=[pl.BlockSpec((tm, o.shape[1]), lambda i: (i, 0)),
                  pl.BlockSpec((tm, d), lambda i: (i, 0)), _resident(w.shape)],
        out_specs=pl.BlockSpec((tm, d), lambda i: (i, 0)),
        out_shape=jax.ShapeDtypeStruct(h.shape, F32),
        compiler_params=_params("parallel"),
        name="out_proj",
    )(o, h, w)


def _ssd_body(x_ref, bc_ref, dt_ref, cw_ref, cb_ref, dtb_ref, alog_ref, dexp_ref, expand_ref,
              y_ref, state_ref, tail_ref, xc_ref):
    q = x_ref.shape[1]
    d_inner = x_ref.shape[2]
    gw = d_inner // SSM_GROUPS
    hpg = gw // SSM_HEADDIM

    @pl.when(pl.program_id(1) == 0)
    def _():
        state_ref[...] = jnp.zeros_like(state_ref)
        tail_ref[...] = jnp.zeros_like(tail_ref)

    for src, cols in ((x_ref, slice(0, d_inner)), (bc_ref, slice(d_inner, cw_ref.shape[1]))):
        cur = src[0]
        prev = tail_ref[:, cols]
        row8 = lax.broadcasted_iota(jnp.int32, prev.shape, 0)
        acc = cb_ref[:, cols] + cw_ref[D_CONV - 1:D_CONV, cols] * cur
        for s in range(1, D_CONV):
            rolled = pltpu.roll(cur, s, axis=0)
            head = jnp.where(row8 < s, pltpu.roll(prev, s, axis=0), rolled[0:SUBLANES])
            back = jnp.concatenate([head, rolled[SUBLANES:]], axis=0)
            acc = acc + cw_ref[D_CONV - 1 - s:D_CONV - s, cols] * back
        xc_ref[:, cols] = acc * jax.nn.sigmoid(acc)
        tail_ref[:, cols] = src[0, q - SUBLANES:q, :]

    v = dt_ref[0] + dtb_ref[...]
    dt = jnp.maximum(v, 0.0) + jnp.log1p(jnp.exp(-jnp.abs(v)))
    a_neg = -jnp.exp(alog_ref[...])
    row = lax.broadcasted_iota(jnp.int32, (q, LANES), 0)
    cum = dt * a_neg
    s = 1
    while s < q:
        cum = cum + jnp.where(row >= s, pltpu.roll(cum, s, axis=0), 0.0)
        s *= 2
    cum_t = cum.T
    dt_t = dt.T
    cum_end = cum[q - 1:q, :]
    ecum_hi, ecum_lo = _split_bf16(jnp.exp(cum))
    wend_hi, wend_lo = _split_bf16(jnp.exp(cum_end - cum) * dt)
    expand = expand_ref[...]
    ecum_x = _dot(ecum_hi, expand) + _dot(ecum_lo, expand)
    wend_x = _dot(wend_hi, expand) + _dot(wend_lo, expand)

    tri = (lax.broadcasted_iota(jnp.int32, (q, q), 0)
           >= lax.broadcasted_iota(jnp.int32, (q, q), 1))
    lane_g = lax.broadcasted_iota(jnp.int32, (q, gw), 1)

    for g in range(SSM_GROUPS):
        xg = xc_ref[:, g * gw:(g + 1) * gw]
        bg = xc_ref[:, d_inner + g * D_STATE:d_inner + (g + 1) * D_STATE].astype(BF16)
        cg = xc_ref[:, d_inner + (SSM_GROUPS + g) * D_STATE:
                    d_inner + (SSM_GROUPS + g + 1) * D_STATE].astype(BF16)
        cb = _dot_nt(cg, bg)
        xg_b = xg.astype(BF16)
        m_parts, x_parts = [], []
        for hh in range(hpg):
            h = g * hpg + hh
            seg = cum[:, h:h + 1] - cum_t[h:h + 1, :]
            decay = jnp.where(tri, jnp.exp(jnp.minimum(seg, 0.0)), 0.0)
            m_parts.append((cb * decay * dt_t[h:h + 1, :]).astype(BF16))
            in_head = (lane_g >= hh * SSM_HEADDIM) & (lane_g < (hh + 1) * SSM_HEADDIM)
            x_parts.append(jnp.where(in_head, xg_b, jnp.zeros_like(xg_b)))
        y_diag = _dot(jnp.concatenate(m_parts, axis=1), jnp.concatenate(x_parts, axis=0))
        st = state_ref[g]
        ecum_g = ecum_x[:, g * gw:(g + 1) * gw]
        y_off = _dot(cg, st.astype(BF16)) * ecum_g
        xw = (xg * wend_x[:, g * gw:(g + 1) * gw]).astype(BF16)
        state_ref[g] = st * ecum_g[q - 1:q, :] + _dot_tn(bg, xw)
        y_ref[0, :, g * gw:(g + 1) * gw] = y_diag + y_off + dexp_ref[:, g * gw:(g + 1) * gw] * xg


def _ssd(zx, dt_raw, conv_w, conv_b, dt_bias, a_log, d_exp, expand, d_inner):
    b, lp, _ = zx.shape
    conv_dim = conv_w.shape[1]
    q = SSD_CHUNK
    gw = d_inner // SSM_GROUPS
    col = pl.BlockSpec((1, q, d_inner), lambda i, c: (i, c, 1))
    col2 = pl.BlockSpec((1, q, conv_dim - d_inner), lambda i, c: (i, c, 2))
    return pl.pallas_call(
        _ssd_body,
        grid=(b, lp // q),
        in_specs=[col, col2, pl.BlockSpec((1, q, LANES), lambda i, c: (i, c, 0)),
                  _resident(conv_w.shape), _resident(conv_b.shape), _resident(dt_bias.shape),
                  _resident(a_log.shape), _resident(d_exp.shape), _resident(expand.shape)],
        out_specs=pl.BlockSpec((1, q, d_inner), lambda i, c: (i, c, 0)),
        out_shape=jax.ShapeDtypeStruct((b, lp, d_inner), F32),
        scratch_shapes=[pltpu.VMEM((SSM_GROUPS, D_STATE, gw), F32),
                        pltpu.VMEM((SUBLANES, conv_dim), F32),
                        pltpu.VMEM((q, conv_dim), F32)],
        compiler_params=_params("parallel", "arbitrary"),
        name="ssd",
    )(zx, zx, dt_raw, conv_w, conv_b, dt_bias, a_log, d_exp, expand)


def _gate_out_body(y_ref, z_ref, h_ref, g_ref, w_ref, o_ref):
    z = z_ref[...]
    y = y_ref[...] * (z * jax.nn.sigmoid(z))
    gw = y.shape[1] // SSM_GROUPS
    parts = []
    for g in range(SSM_GROUPS):
        yg = y[:, g * gw:(g + 1) * gw]
        ms = jnp.mean(yg * yg, axis=-1, keepdims=True)
        parts.append(yg * lax.rsqrt(ms + RMS_EPS))
    yn = (jnp.concatenate(parts, axis=1) * g_ref[...]).astype(BF16)
    o_ref[...] = h_ref[...] + _dot(yn, w_ref[...])


def _gate_out(y, zx, h, norm_g, w):
    rows, d = h.shape
    d_inner = y.shape[1]
    tm = _row_tile(rows)
    return pl.pallas_call(
        _gate_out_body,
        grid=(rows // tm,),
        in_specs=[pl.BlockSpec((tm, d_inner), lambda i: (i, 0)),
                  pl.BlockSpec((tm, d_inner), lambda i: (i, 0)),
                  pl.BlockSpec((tm, d), lambda i: (i, 0)), _resident(norm_g.shape),
                  _resident(w.shape)],
        out_specs=pl.BlockSpec((tm, d), lambda i: (i, 0)),
        out_shape=jax.ShapeDtypeStruct(h.shape, F32),
        compiler_params=_params("parallel"),
        name="gate_out",
    )(y, zx, h, norm_g, w)


def _attn_body(q_ref, k_ref, v_ref, o_ref, r_ref, acc_ref):
    blk = q_ref.shape[1]
    n_pairs = q_ref.shape[2] // LANES
    i = pl.program_id(2)
    lane = lax.broadcasted_iota(jnp.int32, (blk, LANES), 1)
    rr = lax.broadcasted_iota(jnp.int32, (blk, blk), 0)
    cc = lax.broadcasted_iota(jnp.int32, (blk, blk), 1)
    later = (rr > cc).astype(BF16)
    q_heads = []
    for p in range(n_pairs):
        q = q_ref[0, :, p * LANES:(p + 1) * LANES]
        q_heads.append(jnp.where(lane < SB_HEAD_DIM, q, jnp.zeros_like(q)))
        q_heads.append(jnp.where(lane >= SB_HEAD_DIM, q, jnp.zeros_like(q)))
    r_ref[...] = jnp.zeros_like(r_ref)
    acc_ref[...] = jnp.zeros_like(acc_ref)

    def sweep(j, diagonal):
        start = pl.multiple_of(j * blk, blk)
        n_heads = 2 * n_pairs
        zs, log_bs, afters = {}, {}, {}

        def scores(hd):
            kb = k_ref[0, pl.ds(start, blk), (hd // 2) * LANES:(hd // 2 + 1) * LANES]
            zs[hd] = _dot_nt(q_heads[hd], kb)

        def log_terms(hd):
            z = zs.pop(hd)
            sp = jnp.maximum(z, 0.0) + jnp.log(1.0 + jnp.exp2(jnp.abs(z) * -LOG2E))
            drop = jnp.where(cc < rr, sp, 0.0) if diagonal else sp
            afters[hd] = _dot(drop.astype(BF16), later)
            r = r_ref[hd]
            log_bs[hd] = (z - sp) - jnp.concatenate([r] * (blk // LANES), axis=1)
            r_ref[hd] = r + jnp.sum(drop, axis=1, keepdims=True)

        def weigh(hd):
            w = jnp.exp(log_bs.pop(hd) - afters.pop(hd))
            if diagonal:
                w = jnp.where(cc < rr, w, 0.0)
            vb = v_ref[0, pl.ds(start, blk), (hd // 2) * LANES:(hd // 2 + 1) * LANES]
            acc_ref[hd] += _dot(w.astype(BF16), vb)

        for t in range(n_heads + 2 * ATTN_LAG):
            if t < n_heads:
                scores(t)
            if 0 <= t - ATTN_LAG < n_heads:
                log_terms(t - ATTN_LAG)
            if 0 <= t - 2 * ATTN_LAG < n_heads:
                weigh(t - 2 * ATTN_LAG)

    sweep(i, True)

    def cond(carry):
        j, go = carry
        return jnp.logical_and(j >= 0, go)

    def body(carry):
        j, _ = carry
        sweep(j, False)
        return j - 1, jnp.min(r_ref[...]) < -LOG_W_ZERO

    lax.while_loop(cond, body, (i - 1, True))
    for p in range(n_pairs):
        o_ref[0, :, p * LANES:(p + 1) * LANES] = jnp.where(
            lane < SB_HEAD_DIM, acc_ref[2 * p], acc_ref[2 * p + 1]).astype(BF16)


def _attention(q, k, v):
    b, lp, d = q.shape
    blk = ATTN_BLOCK
    n_heads = ATTN_LANES // SB_HEAD_DIM
    blk_spec = pl.BlockSpec((1, blk, ATTN_LANES), lambda bi, p, i: (bi, i, p))
    seq_spec = pl.BlockSpec((1, lp, ATTN_LANES), lambda bi, p, i: (bi, 0, p),
                            pipeline_mode=pl.Buffered(1))
    return pl.pallas_call(
        _attn_body,
        grid=(b, d // ATTN_LANES, lp // blk),
        in_specs=[blk_spec, seq_spec, seq_spec],
        out_specs=blk_spec,
        out_shape=jax.ShapeDtypeStruct(q.shape, BF16),
        scratch_shapes=[pltpu.VMEM((n_heads, blk, LANES), F32),
                        pltpu.VMEM((n_heads, blk, LANES), F32)],
        compiler_params=_params("parallel", "parallel", "arbitrary"),
        name="stick_breaking",
    )(q, k, v)


def kernel(x, meta_tokens, norm_g, ffn_w1, ffn_w3, ffn_w2, ssm_in_proj, ssm_conv_w, ssm_conv_b,
           ssm_dt_bias, ssm_a_log, ssm_d, ssm_norm_g, ssm_out_proj, kv_norm_g, w_k, k_norm_g, w_v,
           sb_w_q, sb_q_norm_g, sb_w_o):
    b, seq, d = x.shape
    depth = norm_g.shape[0]
    n_a = ssm_in_proj.shape[0]
    d_inner = ssm_out_proj.shape[1]
    conv_dim = ssm_conv_w.shape[2]
    n_heads = ssm_dt_bias.shape[1]
    l = N_META + seq
    lp = -(-l // ATTN_BLOCK) * ATTN_BLOCK
    rows = b * lp

    meta = jnp.broadcast_to(meta_tokens[None].astype(x.dtype), (b, N_META, d))
    h = jnp.concatenate([meta, x, jnp.zeros((b, lp - l, d), x.dtype)], axis=1).reshape(rows, d)

    def ffn(h, i, j):
        return _ffn(h, norm_g[i, j * 2][None], ffn_w1[i, j].astype(BF16), ffn_w3[i, j].astype(BF16),
                    ffn_w2[i, j].astype(BF16))

    def pad_lanes(t):
        return jnp.pad(t, [(0, 0), (0, LANES - t.shape[1])])

    head_of_lane = jnp.arange(LANES) // SB_HEAD_DIM
    pool = ((head_of_lane[:, None] == head_of_lane[None, :]).astype(F32) / SB_HEAD_DIM).astype(BF16)
    expand = (jnp.arange(LANES)[:, None] == jnp.arange(d_inner)[None, :] // SSM_HEADDIM).astype(BF16)

    k_shared = v_shared = None
    for i in range(depth):
        h = ffn(h, i, 0)
        g_mix = norm_g[i, 1][None]
        if i < n_a:
            w_in = ssm_in_proj[i]
            zx = _norm_proj(h, g_mix, w_in[:, :d_inner + conv_dim].astype(BF16), d_inner)
            dt_raw = _norm_proj(h, g_mix, pad_lanes(w_in[:, d_inner + conv_dim:]).astype(BF16), LANES)
            y = _ssd(zx.reshape(b, lp, -1), dt_raw.reshape(b, lp, LANES), ssm_conv_w[i],
                     ssm_conv_b[i][None], pad_lanes(ssm_dt_bias[i][None]), pad_lanes(ssm_a_log[i][None]),
                     jnp.repeat(ssm_d[i], SSM_HEADDIM)[None], expand, d_inner)
            h = _gate_out(y.reshape(rows, d_inner), zx, h, ssm_norm_g[i][None],
                          ssm_out_proj[i].astype(BF16))
        else:
            j = i - n_a
            q_gain = jnp.tile(sb_q_norm_g[j], LANES // SB_HEAD_DIM)[None] * SB_HEAD_DIM ** -0.5
            q = _q_proj(h, g_mix, sb_w_q[j].astype(BF16), pool, q_gain)
            o = _attention(q.reshape(b, lp, d), k_shared, v_shared)
            h = _out_proj(o.reshape(rows, d), h, sb_w_o[j].astype(BF16))
        h = ffn(h, i, 1)
        if i == n_a - 1:
            k_gain = jnp.tile(k_norm_g, LANES // SB_HEAD_DIM)[None]
            k2, v2 = _kv_proj(h, kv_norm_g[None], w_k.astype(BF16), w_v.astype(BF16), pool, k_gain)
            k_shared, v_shared = k2.reshape(b, lp, d), v2.reshape(b, lp, d)
    return h.reshape(b, lp, d)[:, N_META:l]
```

```python
import functools

import jax
import jax.numpy as jnp
from jax import lax
from jax.experimental import pallas as pl
from jax.experimental.pallas import tpu as pltpu

F32 = jnp.float32
BF16 = jnp.bfloat16

RMS_EPS = 1e-6
FFN_RES = 0.5
N_META = 16
D_CONV = 4
SSM_HEADDIM = 64
SSM_GROUPS = 8
D_STATE = 128
SB_HEAD_DIM = 64

LANES = 128
SUBLANES = 8
SSD_CHUNK = 128
ATTN_BLOCK = 256
ATTN_LANES = 1024
ROW_TILE = 512
VMEM_LIMIT = 56 * 1024 * 1024

ATTN_LAG = 1
LOG_W_ZERO = -104.0
LOG2E = 1.4426950408889634


def _row_tile(rows, target=ROW_TILE):
    t = min(target, rows)
    while rows % t:
        t -= ATTN_BLOCK
    return t


def _params(*sem):
    return pltpu.CompilerParams(dimension_semantics=sem, vmem_limit_bytes=VMEM_LIMIT)


def _resident(shape):
    nd = len(shape)
    return pl.BlockSpec(shape, lambda *_: (0,) * nd, pipeline_mode=pl.Buffered(1))


def _rms_bf16(x, g):
    ms = jnp.mean(x * x, axis=-1, keepdims=True)
    return (x * lax.rsqrt(ms + RMS_EPS) * g).astype(BF16)


def _split_bf16(x):
    hi = x.astype(BF16)
    lo = (x - hi.astype(F32)).astype(BF16)
    return hi, lo


def _dot(a, b):
    return jnp.dot(a, b, preferred_element_type=F32)


def _dot_nt(a, b):
    return lax.dot_general(a, b, (((1,), (1,)), ((), ())), preferred_element_type=F32)


def _dot_tn(a, b):
    return lax.dot_general(a, b, (((0,), (0,)), ((), ())), preferred_element_type=F32)


def _head_rms(t, pool, gain):
    parts = []
    for p in range(t.shape[1] // LANES):
        tp = t[:, p * LANES:(p + 1) * LANES]
        ms = _dot((tp * tp).astype(BF16), pool)
        parts.append(tp * lax.rsqrt(ms + RMS_EPS) * gain)
    return jnp.concatenate(parts, axis=1)


def _gated_norm(y, z, g):
    y = y * (z * jax.nn.sigmoid(z))
    gw = y.shape[1] // SSM_GROUPS
    parts = []
    for grp in range(SSM_GROUPS):
        yg = y[:, grp * gw:(grp + 1) * gw]
        ms = jnp.mean(yg * yg, axis=-1, keepdims=True)
        parts.append(yg * lax.rsqrt(ms + RMS_EPS))
    return (jnp.concatenate(parts, axis=1) * g).astype(BF16)


def _ffn_body(pre, post, *refs):
    refs = list(refs)
    x = refs.pop(0)[...]
    if pre == "proj":
        o_ref, wo_ref = refs[:2]
        del refs[:2]
        x = x + _dot(o_ref[...], wo_ref[...])
    elif pre == "gate":
        y_ref, z_ref, ng_ref, wout_ref = refs[:4]
        del refs[:4]
        x = x + _dot(_gated_norm(y_ref[...], z_ref[...], ng_ref[...]), wout_ref[...])
    g_ref, w1_ref, w3_ref, w2_ref = refs[:4]
    del refs[:4]
    xn = _rms_bf16(x, g_ref[...])
    a = _dot(xn, w1_ref[...])
    b = _dot(xn, w3_ref[...])
    gate = (a * jax.nn.sigmoid(a) * b).astype(BF16)
    out = x + FFN_RES * _dot(gate, w2_ref[...])
    if post == "q":
        gq_ref, wq_ref, pool_ref, gain_ref, out_ref, q_ref = refs
        q = _dot(_rms_bf16(out, gq_ref[...]), wq_ref[...])
        q_ref[...] = _head_rms(q, pool_ref[...], gain_ref[...]).astype(BF16)
    elif post == "kv":
        gk_ref, wk_ref, wv_ref, pool_ref, gain_ref, out_ref, k_ref, v_ref = refs
        hn = _rms_bf16(out, gk_ref[...])
        k_ref[...] = _head_rms(_dot(hn, wk_ref[...]), pool_ref[...], gain_ref[...]).astype(BF16)
        v_ref[...] = _dot(hn, wv_ref[...]).astype(BF16)
    else:
        (out_ref,) = refs
    out_ref[...] = out


def _ffn(h, ffn_args, pre=None, pre_args=(), post=None, post_args=(), window=None):
    rows, d = h.shape
    n_row_pre = {None: 0, "proj": 1, "gate": 2}[pre]
    tm = _row_tile(rows, ROW_TILE // 2 if pre == "gate" else ROW_TILE)
    if window is None:
        out_rows = rows

        def row_spec(width):
            return pl.BlockSpec((tm, width), lambda i: (i, 0))
    else:
        lp, start, seq = window
        assert seq % tm == 0 and start % (2 * SUBLANES) == 0
        out_rows = rows // lp * seq
        per_seq = seq // tm

        def row_spec(width):
            return pl.BlockSpec((pl.Element(tm), pl.Element(width)),
                                lambda i: (pl.multiple_of((i // per_seq) * lp + start
                                                          + (i % per_seq) * tm, 2 * SUBLANES), 0))
    out_spec = pl.BlockSpec((tm, d), lambda i: (i, 0))
    n_post_out = {None: 0, "q": 1, "kv": 2}[post]
    operands = [h, *pre_args, *ffn_args, *post_args]
    in_specs = ([row_spec(d)] + [row_spec(pre_args[0].shape[1]) for _ in range(n_row_pre)]
                + [_resident(a.shape) for a in operands[1 + n_row_pre:]])
    return pl.pallas_call(
        functools.partial(_ffn_body, pre, post),
        grid=(out_rows // tm,),
        in_specs=in_specs,
        out_specs=[out_spec] * (1 + n_post_out),
        out_shape=[jax.ShapeDtypeStruct((out_rows, d), F32)]
        + [jax.ShapeDtypeStruct((out_rows, d), BF16)] * n_post_out,
        compiler_params=_params("parallel"),
        name="ffn",
    )(*operands)


def _norm_proj_body(h_ref, g_ref, w_ref, o_ref, xn_ref):
    @pl.when(pl.program_id(1) == 0)
    def _():
        xn_ref[...] = _rms_bf16(h_ref[...], g_ref[...])

    o_ref[...] = _dot(xn_ref[...], w_ref[...])


def _norm_proj(h, g, w, tn):
    rows, d = h.shape
    n = w.shape[1]
    tm = _row_tile(rows, 2 * ROW_TILE)
    return pl.pallas_call(
        _norm_proj_body,
        grid=(rows // tm, n // tn),
        in_specs=[pl.BlockSpec((tm, d), lambda i, j: (i, 0)), _resident(g.shape),
                  pl.BlockSpec((d, tn), lambda i, j: (0, j))],
        out_specs=pl.BlockSpec((tm, tn), lambda i, j: (i, j)),
        out_shape=jax.ShapeDtypeStruct((rows, n), F32),
        scratch_shapes=[pltpu.VMEM((tm, d), BF16)],
        compiler_params=_params("parallel", "arbitrary"),
        name="norm_proj",
    )(h, g, w)


def _ssd_body(x_ref, bc_ref, dt_ref, cw_ref, cb_ref, dtb_ref, alog_ref, dexp_ref, expand_ref,
              y_ref, state_ref, tail_ref, xc_ref):
    q = x_ref.shape[1]
    d_inner = x_ref.shape[2]
    gw = d_inner // SSM_GROUPS
    hpg = gw // SSM_HEADDIM

    @pl.when(pl.program_id(1) == 0)
    def _():
        state_ref[...] = jnp.zeros_like(state_ref)
        tail_ref[...] = jnp.zeros_like(tail_ref)

    for src, cols in ((x_ref, slice(0, d_inner)), (bc_ref, slice(d_inner, cw_ref.shape[1]))):
        cur = src[0]
        prev = tail_ref[:, cols]
        row8 = lax.broadcasted_iota(jnp.int32, prev.shape, 0)
        acc = cb_ref[:, cols] + cw_ref[D_CONV - 1:D_CONV, cols] * cur
        for s in range(1, D_CONV):
            rolled = pltpu.roll(cur, s, axis=0)
            head = jnp.where(row8 < s, pltpu.roll(prev, s, axis=0), rolled[0:SUBLANES])
            back = jnp.concatenate([head, rolled[SUBLANES:]], axis=0)
            acc = acc + cw_ref[D_CONV - 1 - s:D_CONV - s, cols] * back
        xc_ref[:, cols] = acc * jax.nn.sigmoid(acc)
        tail_ref[:, cols] = src[0, q - SUBLANES:q, :]

    v = dt_ref[0] + dtb_ref[...]
    dt = jnp.maximum(v, 0.0) + jnp.log1p(jnp.exp(-jnp.abs(v)))
    a_neg = -jnp.exp(alog_ref[...])
    row = lax.broadcasted_iota(jnp.int32, (q, LANES), 0)
    cum = dt * a_neg
    s = 1
    while s < q:
        cum = cum + jnp.where(row >= s, pltpu.roll(cum, s, axis=0), 0.0)
        s *= 2
    cum_t = cum.T
    dt_t = dt.T
    cum_end = cum[q - 1:q, :]
    ecum_hi, ecum_lo = _split_bf16(jnp.exp(cum))
    wend_hi, wend_lo = _split_bf16(jnp.exp(cum_end - cum) * dt)
    expand = expand_ref[...]
    ecum_x = _dot(ecum_hi, expand) + _dot(ecum_lo, expand)
    wend_x = _dot(wend_hi, expand) + _dot(wend_lo, expand)

    tri = (lax.broadcasted_iota(jnp.int32, (q, q), 0)
           >= lax.broadcasted_iota(jnp.int32, (q, q), 1))
    lane_g = lax.broadcasted_iota(jnp.int32, (q, gw), 1)

    for g in range(SSM_GROUPS):
        xg = xc_ref[:, g * gw:(g + 1) * gw]
        bg = xc_ref[:, d_inner + g * D_STATE:d_inner + (g + 1) * D_STATE].astype(BF16)
        cg = xc_ref[:, d_inner + (SSM_GROUPS + g) * D_STATE:
                    d_inner + (SSM_GROUPS + g + 1) * D_STATE].astype(BF16)
        cb = _dot_nt(cg, bg)
        xg_b = xg.astype(BF16)
        m_parts, x_parts = [], []
        for hh in range(hpg):
            h = g * hpg + hh
            seg = cum[:, h:h + 1] - cum_t[h:h + 1, :]
            decay = jnp.where(tri, jnp.exp(jnp.minimum(seg, 0.0)), 0.0)
            m_parts.append((cb * decay * dt_t[h:h + 1, :]).astype(BF16))
            in_head = (lane_g >= hh * SSM_HEADDIM) & (lane_g < (hh + 1) * SSM_HEADDIM)
            x_parts.append(jnp.where(in_head, xg_b, jnp.zeros_like(xg_b)))
        y_diag = _dot(jnp.concatenate(m_parts, axis=1), jnp.concatenate(x_parts, axis=0))
        st = state_ref[g]
        ecum_g = ecum_x[:, g * gw:(g + 1) * gw]
        y_off = _dot(cg, st.astype(BF16)) * ecum_g
        xw = (xg * wend_x[:, g * gw:(g + 1) * gw]).astype(BF16)
        state_ref[g] = st * ecum_g[q - 1:q, :] + _dot_tn(bg, xw)
        y_ref[0, :, g * gw:(g + 1) * gw] = y_diag + y_off + dexp_ref[:, g * gw:(g + 1) * gw] * xg


def _ssd(zx, dt_raw, conv_w, conv_b, dt_bias, a_log, d_exp, expand, d_inner):
    b, lp, _ = zx.shape
    conv_dim = conv_w.shape[1]
    q = SSD_CHUNK
    gw = d_inner // SSM_GROUPS
    col = pl.BlockSpec((1, q, d_inner), lambda i, c: (i, c, 1))
    col2 = pl.BlockSpec((1, q, conv_dim - d_inner), lambda i, c: (i, c, 2))
    return pl.pallas_call(
        _ssd_body,
        grid=(b, lp // q),
        in_specs=[col, col2, pl.BlockSpec((1, q, LANES), lambda i, c: (i, c, 0)),
                  _resident(conv_w.shape), _resident(conv_b.shape), _resident(dt_bias.shape),
                  _resident(a_log.shape), _resident(d_exp.shape), _resident(expand.shape)],
        out_specs=pl.BlockSpec((1, q, d_inner), lambda i, c: (i, c, 0)),
        out_shape=jax.ShapeDtypeStruct((b, lp, d_inner), F32),
        scratch_shapes=[pltpu.VMEM((SSM_GROUPS, D_STATE, gw), F32),
                        pltpu.VMEM((SUBLANES, conv_dim), F32),
                        pltpu.VMEM((q, conv_dim), F32)],
        compiler_params=_params("parallel", "arbitrary"),
        name="ssd",
    )(zx, zx, dt_raw, conv_w, conv_b, dt_bias, a_log, d_exp, expand)


def _attn_body(q_ref, k_ref, v_ref, o_ref, r_ref, acc_ref):
    blk = q_ref.shape[1]
    n_pairs = q_ref.shape[2] // LANES
    i = pl.program_id(2)
    lane = lax.broadcasted_iota(jnp.int32, (blk, LANES), 1)
    rr = lax.broadcasted_iota(jnp.int32, (blk, blk), 0)
    cc = lax.broadcasted_iota(jnp.int32, (blk, blk), 1)
    later = (rr > cc).astype(BF16)
    q_heads = []
    for p in range(n_pairs):
        q = q_ref[0, :, p * LANES:(p + 1) * LANES]
        q_heads.append(jnp.where(lane < SB_HEAD_DIM, q, jnp.zeros_like(q)))
        q_heads.append(jnp.where(lane >= SB_HEAD_DIM, q, jnp.zeros_like(q)))
    r_ref[...] = jnp.zeros_like(r_ref)
    acc_ref[...] = jnp.zeros_like(acc_ref)

    def sweep(j, diagonal):
        start = pl.multiple_of(j * blk, blk)
        n_heads = 2 * n_pairs
        zs, log_bs, afters = {}, {}, {}

        def scores(hd):
            kb = k_ref[0, pl.ds(start, blk), (hd // 2) * LANES:(hd // 2 + 1) * LANES]
            zs[hd] = _dot_nt(q_heads[hd], kb)

        def log_terms(hd):
            z = zs.pop(hd)
            sp = jnp.maximum(z, 0.0) + jnp.log(1.0 + jnp.exp2(jnp.abs(z) * -LOG2E))
            drop = jnp.where(cc < rr, sp, 0.0) if diagonal else sp
            afters[hd] = _dot(drop.astype(BF16), later)
            r = r_ref[hd]
            log_bs[hd] = (z - sp) - jnp.concatenate([r] * (blk // LANES), axis=1)
            r_ref[hd] = r + jnp.sum(drop, axis=1, keepdims=True)

        def weigh(hd):
            w = jnp.exp(log_bs.pop(hd) - afters.pop(hd))
            if diagonal:
                w = jnp.where(cc < rr, w, 0.0)
            vb = v_ref[0, pl.ds(start, blk), (hd // 2) * LANES:(hd // 2 + 1) * LANES]
            acc_ref[hd] += _dot(w.astype(BF16), vb)

        for t in range(n_heads + 2 * ATTN_LAG):
            if t < n_heads:
                scores(t)
            if 0 <= t - ATTN_LAG < n_heads:
                log_terms(t - ATTN_LAG)
            if 0 <= t - 2 * ATTN_LAG < n_heads:
                weigh(t - 2 * ATTN_LAG)

    sweep(i, True)

    def cond(carry):
        j, go = carry
        return jnp.logical_and(j >= 0, go)

    def body(carry):
        j, _ = carry
        sweep(j, False)
        return j - 1, jnp.min(r_ref[...]) < -LOG_W_ZERO

    lax.while_loop(cond, body, (i - 1, True))
    for p in range(n_pairs):
        o_ref[0, :, p * LANES:(p + 1) * LANES] = jnp.where(
            lane < SB_HEAD_DIM, acc_ref[2 * p], acc_ref[2 * p + 1]).astype(BF16)


def _attention(q, k, v):
    b, lp, d = q.shape
    blk = ATTN_BLOCK
    n_heads = ATTN_LANES // SB_HEAD_DIM
    blk_spec = pl.BlockSpec((1, blk, ATTN_LANES), lambda bi, p, i: (bi, i, p))
    seq_spec = pl.BlockSpec((1, lp, ATTN_LANES), lambda bi, p, i: (bi, 0, p),
                            pipeline_mode=pl.Buffered(1))
    return pl.pallas_call(
        _attn_body,
        grid=(b, d // ATTN_LANES, lp // blk),
        in_specs=[blk_spec, seq_spec, seq_spec],
        out_specs=blk_spec,
        out_shape=jax.ShapeDtypeStruct(q.shape, BF16),
        scratch_shapes=[pltpu.VMEM((n_heads, blk, LANES), F32),
                        pltpu.VMEM((n_heads, blk, LANES), F32)],
        compiler_params=_params("parallel", "parallel", "arbitrary"),
        name="stick_breaking",
    )(q, k, v)


def kernel(x, meta_tokens, norm_g, ffn_w1, ffn_w3, ffn_w2, ssm_in_proj, ssm_conv_w, ssm_conv_b,
           ssm_dt_bias, ssm_a_log, ssm_d, ssm_norm_g, ssm_out_proj, kv_norm_g, w_k, k_norm_g, w_v,
           sb_w_q, sb_q_norm_g, sb_w_o):
    b, seq, d = x.shape
    depth = norm_g.shape[0]
    n_a = ssm_in_proj.shape[0]
    d_inner = ssm_out_proj.shape[1]
    conv_dim = ssm_conv_w.shape[2]
    l = N_META + seq
    lp = -(-l // ATTN_BLOCK) * ATTN_BLOCK
    rows = b * lp

    meta = jnp.broadcast_to(meta_tokens[None].astype(x.dtype), (b, N_META, d))
    h = jnp.concatenate([meta, x, jnp.zeros((b, lp - l, d), x.dtype)], axis=1).reshape(rows, d)

    def ffn_args(i, j):
        return (norm_g[i, j * 2][None], ffn_w1[i, j].astype(BF16), ffn_w3[i, j].astype(BF16),
                ffn_w2[i, j].astype(BF16))

    def pad_lanes(t):
        return jnp.pad(t, [(0, 0), (0, LANES - t.shape[1])])

    head_of_lane = jnp.arange(LANES) // SB_HEAD_DIM
    pool = ((head_of_lane[:, None] == head_of_lane[None, :]).astype(F32) / SB_HEAD_DIM).astype(BF16)
    expand = (jnp.arange(LANES)[:, None] == jnp.arange(d_inner)[None, :] // SSM_HEADDIM).astype(BF16)

    k_shared = v_shared = None
    for i in range(depth):
        g_mix = norm_g[i, 1][None]
        if i < n_a:
            (h,) = _ffn(h, ffn_args(i, 0))
            w_in = ssm_in_proj[i]
            zx = _norm_proj(h, g_mix, w_in[:, :d_inner + conv_dim].astype(BF16), d_inner)
            dt_raw = _norm_proj(h, g_mix, pad_lanes(w_in[:, d_inner + conv_dim:]).astype(BF16), LANES)
            y = _ssd(zx.reshape(b, lp, -1), dt_raw.reshape(b, lp, LANES), ssm_conv_w[i],
                     ssm_conv_b[i][None], pad_lanes(ssm_dt_bias[i][None]), pad_lanes(ssm_a_log[i][None]),
                     jnp.repeat(ssm_d[i], SSM_HEADDIM)[None], expand, d_inner)
            pre = "gate"
            pre_args = (y.reshape(rows, d_inner), zx, ssm_norm_g[i][None], ssm_out_proj[i].astype(BF16))
        else:
            j = i - n_a
            q_gain = jnp.tile(sb_q_norm_g[j], LANES // SB_HEAD_DIM)[None] * SB_HEAD_DIM ** -0.5
            h, q = _ffn(h, ffn_args(i, 0), post="q",
                        post_args=(g_mix, sb_w_q[j].astype(BF16), pool, q_gain))
            o = _attention(q.reshape(b, lp, d), k_shared, v_shared)
            pre = "proj"
            pre_args = (o.reshape(rows, d), sb_w_o[j].astype(BF16))
        if i == n_a - 1:
            k_gain = jnp.tile(k_norm_g, LANES // SB_HEAD_DIM)[None]
            h, k2, v2 = _ffn(h, ffn_args(i, 1), pre=pre, pre_args=pre_args, post="kv",
                             post_args=(kv_norm_g[None], w_k.astype(BF16), w_v.astype(BF16), pool, k_gain))
            k_shared, v_shared = k2.reshape(b, lp, d), v2.reshape(b, lp, d)
        elif i == depth - 1:
            (h,) = _ffn(h, ffn_args(i, 1), pre=pre, pre_args=pre_args, window=(lp, N_META, seq))
            return h.reshape(b, seq, d)
        else:
            (h,) = _ffn(h, ffn_args(i, 1), pre=pre, pre_args=pre_args)
    return h.reshape(b, lp, d)[:, N_META:l]
```

```python
import functools

import jax
import jax.numpy as jnp
from jax import lax
from jax.experimental import pallas as pl
from jax.experimental.pallas import tpu as pltpu

F32 = jnp.float32
BF16 = jnp.bfloat16

RMS_EPS = 1e-6
FFN_RES = 0.5
N_META = 16
D_CONV = 4
SSM_HEADDIM = 64
SSM_GROUPS = 8
D_STATE = 128
SB_HEAD_DIM = 64

LANES = 128
SUBLANES = 8
SSD_CHUNK = 128
ATTN_BLOCK = 256
ATTN_LANES = 1024
ROW_TILE = 512
VMEM_LIMIT = 56 * 1024 * 1024

ATTN_LAG = 1
LOG_W_ZERO = -104.0
LOG2E = 1.4426950408889634


def _row_tile(rows, target=ROW_TILE):
    t = min(target, rows)
    while rows % t:
        t -= ATTN_BLOCK
    return t


def _params(*sem):
    return pltpu.CompilerParams(dimension_semantics=sem, vmem_limit_bytes=VMEM_LIMIT)


def _resident(shape):
    nd = len(shape)
    return pl.BlockSpec(shape, lambda *_: (0,) * nd, pipeline_mode=pl.Buffered(1))


def _rms_bf16(x, g):
    ms = jnp.mean(x * x, axis=-1, keepdims=True)
    return (x * lax.rsqrt(ms + RMS_EPS) * g).astype(BF16)


def _split_bf16(x):
    hi = x.astype(BF16)
    lo = (x - hi.astype(F32)).astype(BF16)
    return hi, lo


def _dot(a, b):
    return jnp.dot(a, b, preferred_element_type=F32)


def _dot_nt(a, b):
    return lax.dot_general(a, b, (((1,), (1,)), ((), ())), preferred_element_type=F32)


def _dot_tn(a, b):
    return lax.dot_general(a, b, (((0,), (0,)), ((), ())), preferred_element_type=F32)


def _head_rms(t, pool, gain):
    parts = []
    for p in range(t.shape[1] // LANES):
        tp = t[:, p * LANES:(p + 1) * LANES]
        ms = _dot((tp * tp).astype(BF16), pool)
        parts.append(tp * lax.rsqrt(ms + RMS_EPS) * gain)
    return jnp.concatenate(parts, axis=1)


def _gated_norm(y, z, g):
    y = y * (z * jax.nn.sigmoid(z))
    gw = y.shape[1] // SSM_GROUPS
    parts = []
    for grp in range(SSM_GROUPS):
        yg = y[:, grp * gw:(grp + 1) * gw]
        ms = jnp.mean(yg * yg, axis=-1, keepdims=True)
        parts.append(yg * lax.rsqrt(ms + RMS_EPS))
    return (jnp.concatenate(parts, axis=1) * g).astype(BF16)


def _ffn_body(pre, post, *refs):
    refs = list(refs)
    x = refs.pop(0)[...]
    if pre == "proj":
        o_ref, wo_ref = refs[:2]
        del refs[:2]
        x = x + _dot(o_ref[...], wo_ref[...])
    elif pre == "gate":
        y_ref, z_ref, ng_ref, wout_ref = refs[:4]
        del refs[:4]
        x = x + _dot(_gated_norm(y_ref[...], z_ref[...], ng_ref[...]), wout_ref[...])
    g_ref, w1_ref, w3_ref, w2_ref = refs[:4]
    del refs[:4]
    xn = _rms_bf16(x, g_ref[...])
    a = _dot(xn, w1_ref[...])
    b = _dot(xn, w3_ref[...])
    gate = (a * jax.nn.sigmoid(a) * b).astype(BF16)
    out = x + FFN_RES * _dot(gate, w2_ref[...])
    if post == "q":
        gq_ref, wq_ref, pool_ref, gain_ref, out_ref, q_ref = refs
        q = _dot(_rms_bf16(out, gq_ref[...]), wq_ref[...])
        q_ref[...] = _head_rms(q, pool_ref[...], gain_ref[...]).astype(BF16)
    elif post == "kv":
        gk_ref, wk_ref, wv_ref, pool_ref, gain_ref, out_ref, k_ref, v_ref = refs
        hn = _rms_bf16(out, gk_ref[...])
        k_ref[...] = _head_rms(_dot(hn, wk_ref[...]), pool_ref[...], gain_ref[...]).astype(BF16)
        v_ref[...] = _dot(hn, wv_ref[...]).astype(BF16)
    else:
        (out_ref,) = refs
    out_ref[...] = out


def _ffn(h, ffn_args, pre=None, pre_args=(), post=None, post_args=(), window=None):
    rows, d = h.shape
    n_row_pre = {None: 0, "proj": 1, "gate": 2}[pre]
    tm = _row_tile(rows, ROW_TILE // 2 if pre == "gate" else ROW_TILE)
    if window is None:
        out_rows = rows

        def row_spec(width):
            return pl.BlockSpec((tm, width), lambda i: (i, 0))
    else:
        lp, start, seq = window
        assert seq % tm == 0 and start % (2 * SUBLANES) == 0
        out_rows = rows // lp * seq
        per_seq = seq // tm

        def row_spec(width):
            return pl.BlockSpec((pl.Element(tm), pl.Element(width)),
                                lambda i: (pl.multiple_of((i // per_seq) * lp + start
                                                          + (i % per_seq) * tm, 2 * SUBLANES), 0))
    out_spec = pl.BlockSpec((tm, d), lambda i: (i, 0))
    n_post_out = {None: 0, "q": 1, "kv": 2}[post]
    operands = [h, *pre_args, *ffn_args, *post_args]
    in_specs = ([row_spec(d)] + [row_spec(pre_args[0].shape[1]) for _ in range(n_row_pre)]
                + [_resident(a.shape) for a in operands[1 + n_row_pre:]])
    return pl.pallas_call(
        functools.partial(_ffn_body, pre, post),
        grid=(out_rows // tm,),
        in_specs=in_specs,
        out_specs=[out_spec] * (1 + n_post_out),
        out_shape=[jax.ShapeDtypeStruct((out_rows, d), F32)]
        + [jax.ShapeDtypeStruct((out_rows, d), BF16)] * n_post_out,
        compiler_params=_params("parallel"),
        name="ffn",
    )(*operands)


def _in_proj_body(h_ref, g_ref, w_ref, wdt_ref, zx_ref, dt_ref):
    xn = _rms_bf16(h_ref[...], g_ref[...])
    zx_ref[...] = _dot(xn, w_ref[...])
    dt_ref[...] = _dot(xn, wdt_ref[...])


def _in_proj(h, g, w, w_dt):
    rows, d = h.shape
    tm = _row_tile(rows)
    return pl.pallas_call(
        _in_proj_body,
        grid=(rows // tm,),
        in_specs=[pl.BlockSpec((tm, d), lambda i: (i, 0)), _resident(g.shape),
                  _resident(w.shape), _resident(w_dt.shape)],
        out_specs=[pl.BlockSpec((tm, w.shape[1]), lambda i: (i, 0)),
                   pl.BlockSpec((tm, w_dt.shape[1]), lambda i: (i, 0))],
        out_shape=[jax.ShapeDtypeStruct((rows, w.shape[1]), F32),
                   jax.ShapeDtypeStruct((rows, w_dt.shape[1]), F32)],
        compiler_params=_params("parallel"),
        name="in_proj",
    )(h, g, w, w_dt)


def _ssd_body(x_ref, bc_ref, dt_ref, cw_ref, cb_ref, dtb_ref, alog_ref, dexp_ref, expand_ref,
              y_ref, state_ref, tail_ref, xc_ref):
    q = x_ref.shape[1]
    d_inner = x_ref.shape[2]
    gw = d_inner // SSM_GROUPS
    hpg = gw // SSM_HEADDIM

    @pl.when(pl.program_id(1) == 0)
    def _():
        state_ref[...] = jnp.zeros_like(state_ref)
        tail_ref[...] = jnp.zeros_like(tail_ref)

    for src, cols in ((x_ref, slice(0, d_inner)), (bc_ref, slice(d_inner, cw_ref.shape[1]))):
        cur = src[0]
        prev = tail_ref[:, cols]
        row8 = lax.broadcasted_iota(jnp.int32, prev.shape, 0)
        acc = cb_ref[:, cols] + cw_ref[D_CONV - 1:D_CONV, cols] * cur
        for s in range(1, D_CONV):
            rolled = pltpu.roll(cur, s, axis=0)
            head = jnp.where(row8 < s, pltpu.roll(prev, s, axis=0), rolled[0:SUBLANES])
            back = jnp.concatenate([head, rolled[SUBLANES:]], axis=0)
            acc = acc + cw_ref[D_CONV - 1 - s:D_CONV - s, cols] * back
        xc_ref[:, cols] = acc * jax.nn.sigmoid(acc)
        tail_ref[:, cols] = src[0, q - SUBLANES:q, :]

    v = dt_ref[0] + dtb_ref[...]
    dt = jnp.maximum(v, 0.0) + jnp.log(1.0 + jnp.exp(-jnp.abs(v)))
    a_neg = -jnp.exp(alog_ref[...])
    row = lax.broadcasted_iota(jnp.int32, (q, LANES), 0)
    cum = dt * a_neg
    s = 1
    while s < q:
        cum = cum + jnp.where(row >= s, pltpu.roll(cum, s, axis=0), 0.0)
        s *= 2
    cum_t = cum.T
    dt_t = dt.T
    cum_end = cum[q - 1:q, :]
    ecum_hi, ecum_lo = _split_bf16(jnp.exp(cum))
    wend_hi, wend_lo = _split_bf16(jnp.exp(cum_end - cum) * dt)
    expand = expand_ref[...]
    ecum_x = _dot(ecum_hi, expand) + _dot(ecum_lo, expand)
    wend_x = _dot(wend_hi, expand) + _dot(wend_lo, expand)

    tri = (lax.broadcasted_iota(jnp.int32, (q, q), 0)
           >= lax.broadcasted_iota(jnp.int32, (q, q), 1))
    lane_g = lax.broadcasted_iota(jnp.int32, (q, gw), 1)

    for g in range(SSM_GROUPS):
        xg = xc_ref[:, g * gw:(g + 1) * gw]
        bg = xc_ref[:, d_inner + g * D_STATE:d_inner + (g + 1) * D_STATE].astype(BF16)
        cg = xc_ref[:, d_inner + (SSM_GROUPS + g) * D_STATE:
                    d_inner + (SSM_GROUPS + g + 1) * D_STATE].astype(BF16)
        cb = _dot_nt(cg, bg)
        xg_b = xg.astype(BF16)
        m_parts, x_parts = [], []
        for hh in range(hpg):
            h = g * hpg + hh
            seg = cum[:, h:h + 1] - cum_t[h:h + 1, :]
            decay = jnp.where(tri, jnp.exp(jnp.minimum(seg, 0.0)), 0.0)
            m_parts.append((cb * decay * dt_t[h:h + 1, :]).astype(BF16))
            in_head = (lane_g >= hh * SSM_HEADDIM) & (lane_g < (hh + 1) * SSM_HEADDIM)
            x_parts.append(jnp.where(in_head, xg_b, jnp.zeros_like(xg_b)))
        y_diag = _dot(jnp.concatenate(m_parts, axis=1), jnp.concatenate(x_parts, axis=0))
        st = state_ref[g]
        ecum_g = ecum_x[:, g * gw:(g + 1) * gw]
        y_off = _dot(cg, st.astype(BF16)) * ecum_g
        xw = (xg * wend_x[:, g * gw:(g + 1) * gw]).astype(BF16)
        state_ref[g] = st * ecum_g[q - 1:q, :] + _dot_tn(bg, xw)
        y_ref[0, :, g * gw:(g + 1) * gw] = y_diag + y_off + dexp_ref[:, g * gw:(g + 1) * gw] * xg


def _ssd(zx, dt_raw, conv_w, conv_b, dt_bias, a_log, d_exp, expand, d_inner):
    b, lp, _ = zx.shape
    conv_dim = conv_w.shape[1]
    q = SSD_CHUNK
    gw = d_inner // SSM_GROUPS
    col = pl.BlockSpec((1, q, d_inner), lambda i, c: (i, c, 1))
    col2 = pl.BlockSpec((1, q, conv_dim - d_inner), lambda i, c: (i, c, 2))
    return pl.pallas_call(
        _ssd_body,
        grid=(b, lp // q),
        in_specs=[col, col2, pl.BlockSpec((1, q, LANES), lambda i, c: (i, c, 0)),
                  _resident(conv_w.shape), _resident(conv_b.shape), _resident(dt_bias.shape),
                  _resident(a_log.shape), _resident(d_exp.shape), _resident(expand.shape)],
        out_specs=pl.BlockSpec((1, q, d_inner), lambda i, c: (i, c, 0)),
        out_shape=jax.ShapeDtypeStruct((b, lp, d_inner), F32),
        scratch_shapes=[pltpu.VMEM((SSM_GROUPS, D_STATE, gw), F32),
                        pltpu.VMEM((SUBLANES, conv_dim), F32),
                        pltpu.VMEM((q, conv_dim), F32)],
        compiler_params=_params("parallel", "arbitrary"),
        name="ssd",
    )(zx, zx, dt_raw, conv_w, conv_b, dt_bias, a_log, d_exp, expand)


def _attn_body(q_ref, k_ref, v_ref, o_ref, r_ref, acc_ref):
    blk = q_ref.shape[1]
    n_pairs = q_ref.shape[2] // LANES
    i = pl.program_id(2)
    lane = lax.broadcasted_iota(jnp.int32, (blk, LANES), 1)
    rr = lax.broadcasted_iota(jnp.int32, (blk, blk), 0)
    cc = lax.broadcasted_iota(jnp.int32, (blk, blk), 1)
    later = (rr > cc).astype(BF16)
    q_heads = []
    for p in range(n_pairs):
        q = q_ref[0, :, p * LANES:(p + 1) * LANES]
        q_heads.append(jnp.where(lane < SB_HEAD_DIM, q, jnp.zeros_like(q)))
        q_heads.append(jnp.where(lane >= SB_HEAD_DIM, q, jnp.zeros_like(q)))
    r_ref[...] = jnp.zeros_like(r_ref)
    acc_ref[...] = jnp.zeros_like(acc_ref)

    def sweep(j, diagonal):
        start = pl.multiple_of(j * blk, blk)
        n_heads = 2 * n_pairs
        zs, log_bs, afters = {}, {}, {}

        def scores(hd):
            kb = k_ref[0, pl.ds(start, blk), (hd // 2) * LANES:(hd // 2 + 1) * LANES]
            zs[hd] = _dot_nt(q_heads[hd], kb)

        def log_terms(hd):
            z = zs.pop(hd)
            sp = jnp.maximum(z, 0.0) + jnp.log(1.0 + jnp.exp2(jnp.abs(z) * -LOG2E))
            drop = jnp.where(cc < rr, sp, 0.0) if diagonal else sp
            afters[hd] = _dot(drop.astype(BF16), later)
            r = r_ref[hd]
            log_bs[hd] = (z - sp) - jnp.concatenate([r] * (blk // LANES), axis=1)
            r_ref[hd] = r + jnp.sum(drop, axis=1, keepdims=True)

        def weigh(hd):
            w = jnp.exp(log_bs.pop(hd) - afters.pop(hd))
            if diagonal:
                w = jnp.where(cc < rr, w, 0.0)
            vb = v_ref[0, pl.ds(start, blk), (hd // 2) * LANES:(hd // 2 + 1) * LANES]
            acc_ref[hd] += _dot(w.astype(BF16), vb)

        for t in range(n_heads + 2 * ATTN_LAG):
            if t < n_heads:
                scores(t)
            if 0 <= t - ATTN_LAG < n_heads:
                log_terms(t - ATTN_LAG)
            if 0 <= t - 2 * ATTN_LAG < n_heads:
                weigh(t - 2 * ATTN_LAG)

    sweep(i, True)

    def cond(carry):
        j, go = carry
        return jnp.logical_and(j >= 0, go)

    def body(carry):
        j, _ = carry
        sweep(j, False)
        return j - 1, jnp.min(r_ref[...]) < -LOG_W_ZERO

    lax.while_loop(cond, body, (i - 1, True))
    for p in range(n_pairs):
        o_ref[0, :, p * LANES:(p + 1) * LANES] = jnp.where(
            lane < SB_HEAD_DIM, acc_ref[2 * p], acc_ref[2 * p + 1]).astype(BF16)


def _attention(q, k, v):
    b, lp, d = q.shape
    blk = ATTN_BLOCK
    n_heads = ATTN_LANES // SB_HEAD_DIM
    blk_spec = pl.BlockSpec((1, blk, ATTN_LANES), lambda bi, p, i: (bi, i, p))
    seq_spec = pl.BlockSpec((1, lp, ATTN_LANES), lambda bi, p, i: (bi, 0, p),
                            pipeline_mode=pl.Buffered(1))
    return pl.pallas_call(
        _attn_body,
        grid=(b, d // ATTN_LANES, lp // blk),
        in_specs=[blk_spec, seq_spec, seq_spec],
        out_specs=blk_spec,
        out_shape=jax.ShapeDtypeStruct(q.shape, BF16),
        scratch_shapes=[pltpu.VMEM((n_heads, blk, LANES), F32),
                        pltpu.VMEM((n_heads, blk, LANES), F32)],
        compiler_params=_params("parallel", "parallel", "arbitrary"),
        name="stick_breaking",
    )(q, k, v)


def kernel(x, meta_tokens, norm_g, ffn_w1, ffn_w3, ffn_w2, ssm_in_proj, ssm_conv_w, ssm_conv_b,
           ssm_dt_bias, ssm_a_log, ssm_d, ssm_norm_g, ssm_out_proj, kv_norm_g, w_k, k_norm_g, w_v,
           sb_w_q, sb_q_norm_g, sb_w_o):
    b, seq, d = x.shape
    depth = norm_g.shape[0]
    n_a = ssm_in_proj.shape[0]
    d_inner = ssm_out_proj.shape[1]
    conv_dim = ssm_conv_w.shape[2]
    l = N_META + seq
    lp = -(-l // ATTN_BLOCK) * ATTN_BLOCK
    rows = b * lp

    meta = jnp.broadcast_to(meta_tokens[None].astype(x.dtype), (b, N_META, d))
    h = jnp.concatenate([meta, x, jnp.zeros((b, lp - l, d), x.dtype)], axis=1).reshape(rows, d)

    def ffn_args(i, j):
        return (norm_g[i, j * 2][None], ffn_w1[i, j].astype(BF16), ffn_w3[i, j].astype(BF16),
                ffn_w2[i, j].astype(BF16))

    def pad_lanes(t):
        return jnp.pad(t, [(0, 0), (0, LANES - t.shape[1])])

    head_of_lane = jnp.arange(LANES) // SB_HEAD_DIM
    pool = ((head_of_lane[:, None] == head_of_lane[None, :]).astype(F32) / SB_HEAD_DIM).astype(BF16)
    expand = (jnp.arange(LANES)[:, None] == jnp.arange(d_inner)[None, :] // SSM_HEADDIM).astype(BF16)

    k_shared = v_shared = None
    for i in range(depth):
        g_mix = norm_g[i, 1][None]
        if i < n_a:
            (h,) = _ffn(h, ffn_args(i, 0))
            w_in = ssm_in_proj[i]
            zx, dt_raw = _in_proj(h, g_mix, w_in[:, :d_inner + conv_dim].astype(BF16),
                                  pad_lanes(w_in[:, d_inner + conv_dim:]).astype(BF16))
            y = _ssd(zx.reshape(b, lp, -1), dt_raw.reshape(b, lp, LANES), ssm_conv_w[i],
                     ssm_conv_b[i][None], pad_lanes(ssm_dt_bias[i][None]), pad_lanes(ssm_a_log[i][None]),
                     jnp.repeat(ssm_d[i], SSM_HEADDIM)[None], expand, d_inner)
            pre = "gate"
            pre_args = (y.reshape(rows, d_inner), zx, ssm_norm_g[i][None], ssm_out_proj[i].astype(BF16))
        else:
            j = i - n_a
            q_gain = jnp.tile(sb_q_norm_g[j], LANES // SB_HEAD_DIM)[None] * SB_HEAD_DIM ** -0.5
            h, q = _ffn(h, ffn_args(i, 0), post="q",
                        post_args=(g_mix, sb_w_q[j].astype(BF16), pool, q_gain))
            o = _attention(q.reshape(b, lp, d), k_shared, v_shared)
            pre = "proj"
            pre_args = (o.reshape(rows, d), sb_w_o[j].astype(BF16))
        if i == n_a - 1:
            k_gain = jnp.tile(k_norm_g, LANES // SB_HEAD_DIM)[None]
            h, k2, v2 = _ffn(h, ffn_args(i, 1), pre=pre, pre_args=pre_args, post="kv",
                             post_args=(kv_norm_g[None], w_k.astype(BF16), w_v.astype(BF16), pool, k_gain))
            k_shared, v_shared = k2.reshape(b, lp, d), v2.reshape(b, lp, d)
        elif i == depth - 1:
            (h,) = _ffn(h, ffn_args(i, 1), pre=pre, pre_args=pre_args, window=(lp, N_META, seq))
            return h.reshape(b, seq, d)
        else:
            (h,) = _ffn(h, ffn_args(i, 1), pre=pre, pre_args=pre_args)
    return h.reshape(b, lp, d)[:, N_META:l]
```
